```python
import math
import jax, jax.numpy as jnp
from jax import lax
import numpy as np

D_MODEL = 1024
BATCH = 4
SEQ = 4096
DEPTH = 2

CHUNK = 64
HEAD_DIM = 64
D_POOL = D_MODEL // 4
POOL_WINDOWS = (2, 4, 8, 16)
N_POOL_GROUPS = len(POOL_WINDOWS)
POOL_GROUP = D_POOL // N_POOL_GROUPS
D_CHUNK = 3 * D_MODEL // 8
N_CHUNK_HEADS = D_CHUNK // HEAD_DIM
LEFT_CHUNKS = 8
BAND = (LEFT_CHUNKS + 1) * CHUNK
REL_CLIP = 128
D_SB = D_MODEL - D_POOL - D_CHUNK
N_SB_HEADS = D_SB // HEAD_DIM
SB_BLOCK = 128
D_IN = D_POOL + 3 * D_CHUNK + 3 * D_SB
D_FF = ((8 * D_MODEL // 3 + 127) // 128) * 128
N_EXPERTS = 8
TOP_K = 2
D_FF_EXPERT = D_FF
N_DENSE = (DEPTH + 1) // 2
N_MOE = DEPTH // 2
ALPHA = (2.0 * DEPTH) ** 0.25
BETA = (8.0 * DEPTH) ** -0.25
LN_EPS = 1e-5
RMS_EPS = 1e-6
NEG_INF = -1e30

kernel_name = "hybrid_pool_chunkattn_stickbreak_moe_deepnorm"


def layer_norm(x, g, b):
    xf = x.astype(jnp.float32)
    mu = jnp.mean(xf, axis=-1, keepdims=True)
    xc = xf - mu
    var = jnp.mean(xc * xc, axis=-1, keepdims=True)
    y = xc * lax.rsqrt(var + LN_EPS) * g.astype(jnp.float32) + b.astype(jnp.float32)
    return y.astype(x.dtype)


def rms_normalise(y):
    yf = y.astype(jnp.float32)
    return yf * lax.rsqrt(jnp.mean(yf * yf, axis=-1, keepdims=True) + RMS_EPS)


def pool_mixer(u, pool_w, pool_scale):
    B, S, _ = u.shape
    uf = u.astype(jnp.float32)
    cs = jnp.cumsum(uf, axis=1)
    t = jnp.arange(S, dtype=jnp.float32)[None, :, None]
    outs = []
    for g, w in enumerate(POOL_WINDOWS):
        c = cs[..., g * POOL_GROUP:(g + 1) * POOL_GROUP]
        lag = jnp.pad(c[:, :-w], ((0, 0), (w, 0), (0, 0)))
        cnt = jnp.minimum(t + 1.0, float(w))
        outs.append((c - lag) / cnt)
    pooled = jnp.concatenate(outs, axis=-1) - uf
    pooled = pooled.reshape(B, S, N_POOL_GROUPS, POOL_GROUP)
    y = jnp.einsum('bsgc,gcd->bsgd', pooled, pool_w.astype(jnp.float32))
    return y.reshape(B, S, D_POOL) * pool_scale.astype(jnp.float32)


def chunk_attention(q, k, v, rel_bias):
    B, S, H, d = q.shape
    nc = S // CHUNK
    qc = q.reshape(B, nc, CHUNK, H, d).astype(jnp.float32)
    pad = ((0, 0), (LEFT_CHUNKS, 0), (0, 0), (0, 0), (0, 0))
    kc = jnp.pad(k.reshape(B, nc, CHUNK, H, d).astype(jnp.float32), pad)
    vc = jnp.pad(v.reshape(B, nc, CHUNK, H, d).astype(jnp.float32), pad)
    kband = jnp.concatenate([kc[:, j:j + nc] for j in range(LEFT_CHUNKS + 1)], axis=2)
    vband = jnp.concatenate([vc[:, j:j + nc] for j in range(LEFT_CHUNKS + 1)], axis=2)
    scores = jnp.einsum('bnqhd,bnkhd->bhnqk', qc, kband) * (1.0 / math.sqrt(d))
    qi = jnp.arange(CHUNK)[:, None]
    kj = jnp.arange(BAND)[None, :]
    rel = jnp.clip(qi + LEFT_CHUNKS * CHUNK - kj, -REL_CLIP, REL_CLIP) + REL_CLIP
    bias = rel_bias.astype(jnp.float32)[:, rel]
    key_chunk = jnp.arange(nc)[:, None] - LEFT_CHUNKS + jnp.arange(BAND)[None, :] // CHUNK
    valid = key_chunk >= 0
    scores = scores + bias[None, :, None, :, :]
    scores = jnp.where(valid[None, None, :, None, :], scores, NEG_INF)
    p = jax.nn.softmax(scores, axis=-1)
    o = jnp.einsum('bhnqk,bnkhd->bnqhd', p, vband)
    return o.reshape(B, S, H * d)


def stick_breaking_attention(q, k, v):
    B, S, H, d = q.shape
    qh = q.transpose(0, 2, 1, 3).astype(jnp.float32)
    kh = k.transpose(0, 2, 1, 3).astype(jnp.float32)
    vh = v.transpose(0, 2, 1, 3).astype(jnp.float32)
    scale = 1.0 / math.sqrt(d)
    outs = []
    for blk in range(S // SB_BLOCK):
        lo = blk * SB_BLOCK
        hi = lo + SB_BLOCK
        z = jnp.einsum('bhqd,bhkd->bhqk', qh[:, :, lo:hi], kh[:, :, :hi]) * scale
        t = lo + jnp.arange(SB_BLOCK)[:, None]
        s = jnp.arange(hi)[None, :]
        before = s < t
        log_keep = jnp.where(before, jax.nn.log_sigmoid(-z), 0.0)
        suffix = lax.cumsum(log_keep, axis=3, reverse=True) - log_keep
        w = jnp.where(before, jnp.exp(jax.nn.log_sigmoid(z) + suffix), 0.0)
        outs.append(jnp.einsum('bhqk,bhkd->bhqd', w, vh[:, :, :hi]))
    o = jnp.concatenate(outs, axis=2)
    return o.transpose(0, 2, 1, 3).reshape(B, S, H * d)


def hybrid_mixer(x, w_in, pool_w, pool_scale, rel_bias, g_mix, w_out):
    B, S, _ = x.shape
    h = x @ w_in
    u_pool = h[..., :D_POOL]
    qkv_c = h[..., D_POOL:D_POOL + 3 * D_CHUNK]
    qkv_s = h[..., D_POOL + 3 * D_CHUNK:]
    y_pool = pool_mixer(u_pool, pool_w, pool_scale)
    qc, kc, vc = [a.reshape(B, S, N_CHUNK_HEADS, HEAD_DIM) for a in jnp.split(qkv_c, 3, axis=-1)]
    y_chunk = chunk_attention(qc, kc, vc, rel_bias)
    qs, ks, vs = [a.reshape(B, S, N_SB_HEADS, HEAD_DIM) for a in jnp.split(qkv_s, 3, axis=-1)]
    y_sb = stick_breaking_attention(qs, ks, vs)
    y = jnp.concatenate([rms_normalise(y_pool), rms_normalise(y_chunk), rms_normalise(y_sb)], axis=-1)
    y = (y * g_mix.astype(jnp.float32)).astype(x.dtype)
    return y @ w_out


def swiglu(x, wg, wu, wd):
    return (jax.nn.silu(x @ wg) * (x @ wu)) @ wd


def moe_swiglu(x, router, wg, wu, wd):
    B, S, D = x.shape
    xt = x.reshape(B * S, D)
    logits = (xt @ router).astype(jnp.float32)
    top_val, top_idx = lax.top_k(logits, TOP_K)
    top_w = jax.nn.softmax(top_val, axis=-1)
    gates = jnp.sum(jax.nn.one_hot(top_idx, N_EXPERTS, dtype=jnp.float32) * top_w[..., None], axis=1)
    out = jnp.zeros((B * S, D), jnp.float32)
    for e in range(N_EXPERTS):
        out = out + gates[:, e:e + 1] * swiglu(xt, wg[e], wu[e], wd[e]).astype(jnp.float32)
    return out.reshape(B, S, D).astype(x.dtype)


def setup_inputs(seed: int = 0) -> dict:
    key = jax.random.key(seed)
    ks = jax.random.split(key, 20)
    f32 = jnp.float32
    n = lambda k, shape, s: jax.random.normal(k, shape, f32) * s
    x = jax.random.normal(ks[0], (BATCH, SEQ, D_MODEL), f32)
    w_in = n(ks[1], (DEPTH, D_MODEL, D_IN), D_MODEL ** -0.5)
    pool_w = n(ks[2], (DEPTH, N_POOL_GROUPS, POOL_GROUP, POOL_GROUP), POOL_GROUP ** -0.5)
    pool_scale = 1.0 + n(ks[3], (DEPTH, D_POOL), 0.05)
    rel_bias = n(ks[4], (DEPTH, N_CHUNK_HEADS, 2 * REL_CLIP + 1), 0.3)
    g_mix = 1.0 + n(ks[5], (DEPTH, D_MODEL), 0.05)
    w_out = n(ks[6], (DEPTH, D_MODEL, D_MODEL), BETA * D_MODEL ** -0.5)
    ln1_g = 1.0 + n(ks[7], (DEPTH, D_MODEL), 0.05)
    ln1_b = n(ks[8], (DEPTH, D_MODEL), 0.02)
    ln2_g = 1.0 + n(ks[9], (DEPTH, D_MODEL), 0.05)
    ln2_b = n(ks[10], (DEPTH, D_MODEL), 0.02)
    ffn_wg = n(ks[11], (N_DENSE, D_MODEL, D_FF), D_MODEL ** -0.5)
    ffn_wu = n(ks[12], (N_DENSE, D_MODEL, D_FF), D_MODEL ** -0.5)
    ffn_wd = n(ks[13], (N_DENSE, D_FF, D_MODEL), BETA * D_FF ** -0.5)
    moe_router = n(ks[14], (N_MOE, D_MODEL, N_EXPERTS), D_MODEL ** -0.5)
    moe_wg = n(ks[15], (N_MOE, N_EXPERTS, D_MODEL, D_FF_EXPERT), D_MODEL ** -0.5)
    moe_wu = n(ks[16], (N_MOE, N_EXPERTS, D_MODEL, D_FF_EXPERT), D_MODEL ** -0.5)
    moe_wd = n(ks[17], (N_MOE, N_EXPERTS, D_FF_EXPERT, D_MODEL), BETA * D_FF_EXPERT ** -0.5)
    return {"x": x, "w_in": w_in, "pool_w": pool_w, "pool_scale": pool_scale, "rel_bias": rel_bias,
            "g_mix": g_mix, "w_out": w_out, "ln1_g": ln1_g, "ln1_b": ln1_b, "ln2_g": ln2_g, "ln2_b": ln2_b,
            "ffn_wg": ffn_wg, "ffn_wu": ffn_wu, "ffn_wd": ffn_wd, "moe_router": moe_router,
            "moe_wg": moe_wg, "moe_wu": moe_wu, "moe_wd": moe_wd}


def reference(x, w_in, pool_w, pool_scale, rel_bias, g_mix, w_out, ln1_g, ln1_b, ln2_g, ln2_b,
              ffn_wg, ffn_wu, ffn_wd, moe_router, moe_wg, moe_wu, moe_wd):
    for i in range(DEPTH):
        m = hybrid_mixer(x, w_in[i], pool_w[i], pool_scale[i], rel_bias[i], g_mix[i], w_out[i])
        x = layer_norm(ALPHA * x + m, ln1_g[i], ln1_b[i])
        j = i // 2
        if i % 2 == 0:
            f = swiglu(x, ffn_wg[j], ffn_wu[j], ffn_wd[j])
        else:
            f = moe_swiglu(x, moe_router[j], moe_wg[j], moe_wu[j], moe_wd[j])
        x = layer_norm(ALPHA * x + f, ln2_g[i], ln2_b[i])
    return x
```

```python
import functools
import math

import jax
import jax.numpy as jnp
from jax import lax
from jax.experimental import pallas as pl
from jax.experimental.pallas import tpu as pltpu

F32 = jnp.float32
BF16 = jnp.bfloat16

DEPTH = 2
HEAD_DIM = 64
CHUNK = 64
LEFT_CHUNKS = 8
REL_CLIP = 128
POOL_WINDOWS = (2, 4, 8, 16)
POOL_GROUP = 64
D_POOL = 256
D_CHUNK = 384
D_SB = 384
N_EXPERTS = 8
ALPHA = (2.0 * DEPTH) ** 0.25
LN_EPS = 1e-5
RMS_EPS = 1e-6
NEG_INF = -1e30

LANES = 128
HEADS_PER_LANE_TILE = LANES // HEAD_DIM
VMEM_LIMIT = 56 * 1024 * 1024

CA_Q = 4 * CHUNK
CA_K = (LEFT_CHUNKS + 4) * CHUNK
SB_T = 128
SB_DEAD = -104.0
MOE_TM = 1024
MOE_RT = 256


def _cparams(sem, vmem=VMEM_LIMIT):
    return pltpu.CompilerParams(dimension_semantics=sem, vmem_limit_bytes=vmem)


def _nt_dot(a, b):
    return lax.dot_general(a, b, (((1,), (1,)), ((), ())), preferred_element_type=F32)


def _layer_norm(r, g, b):
    mu = jnp.mean(r, axis=-1, keepdims=True)
    rc = r - mu
    var = jnp.mean(rc * rc, axis=-1, keepdims=True)
    return rc * lax.rsqrt(var + LN_EPS) * g + b


def _inproj_kernel(x_ref, w_ref, u_ref, qc_ref, kc_ref, vc_ref, qs_ref, ks_ref, vs_ref):
    xb = x_ref[...].astype(BF16)
    scale = 1.0 / math.sqrt(HEAD_DIM)

    def mm(lo, width):
        return jnp.dot(xb, w_ref[:, lo:lo + width], preferred_element_type=F32)

    u_ref[...] = mm(0, D_POOL)
    o = D_POOL
    qc_ref[...] = (mm(o, D_CHUNK) * scale).astype(BF16)
    kc_ref[...] = mm(o + D_CHUNK, D_CHUNK).astype(BF16)
    vc_ref[...] = mm(o + 2 * D_CHUNK, D_CHUNK).astype(BF16)
    o = D_POOL + 3 * D_CHUNK
    qs_ref[...] = (mm(o, D_SB) * scale).astype(BF16)
    ks_ref[...] = mm(o + D_SB, D_SB).astype(BF16)
    vs_ref[...] = mm(o + 2 * D_SB, D_SB).astype(BF16)


def _inproj(x2, w_bf, tm=512):
    T, D = x2.shape
    d_in = w_bf.shape[1]
    row = lambda w: pl.BlockSpec((tm, w), lambda i: (i, 0))
    shapes = [jax.ShapeDtypeStruct((T, D_POOL), F32)] + [jax.ShapeDtypeStruct((T, D_CHUNK), BF16)] * 6
    return pl.pallas_call(
        _inproj_kernel,
        grid=(T // tm,),
        in_specs=[row(D), pl.BlockSpec((D, d_in), lambda i: (0, 0))],
        out_specs=[row(D_POOL)] + [row(D_CHUNK)] * 6,
        out_shape=shapes,
        compiler_params=_cparams(("parallel",)),
        name="inproj",
    )(x2, w_bf)


POOL_HALO = 16


def _pool_kernel(u_ref, w_ref, scale_ref, g_ref, o_ref, ext_ref):
    ts = u_ref.shape[1]
    s = pl.program_id(1)

    @pl.when(s == 0)
    def _():
        ext_ref[0:POOL_HALO, :] = jnp.zeros((POOL_HALO, D_POOL), F32)

    u = u_ref[0]
    ext_ref[POOL_HALO:POOL_HALO + ts, :] = u
    lane = lax.broadcasted_iota(jnp.int32, (1, D_POOL), 1)
    win = jnp.left_shift(2, lane // POOL_GROUP)
    acc = u
    for j in range(1, POOL_HALO):
        sh = ext_ref[POOL_HALO - j:POOL_HALO - j + ts, :]
        acc = acc + jnp.where(win > j, sh, 0.0)
    t = s * ts + lax.broadcasted_iota(jnp.int32, (ts, 1), 0)
    cnt = jnp.minimum(t + 1, win).astype(F32)
    pooled = acc / cnt - u
    y = jnp.dot(pooled.astype(BF16), w_ref[...], preferred_element_type=F32) * scale_ref[...]
    yn = y * lax.rsqrt(jnp.mean(y * y, axis=-1, keepdims=True) + RMS_EPS)
    o_ref[0] = (yn * g_ref[...]).astype(o_ref.dtype)
    ext_ref[0:POOL_HALO, :] = ext_ref[ts:ts + POOL_HALO, :]


def _pool(u3, w_bd, scale, g, ts=512):
    B, S, _ = u3.shape
    ts = min(ts, S)
    vec = pl.BlockSpec((1, D_POOL), lambda b, s: (0, 0))
    return pl.pallas_call(
        _pool_kernel,
        grid=(B, S // ts),
        in_specs=[pl.BlockSpec((1, ts, D_POOL), lambda b, s: (b, s, 0)),
                  pl.BlockSpec((D_POOL, D_POOL), lambda b, s: (0, 0)), vec, vec],
        out_specs=pl.BlockSpec((1, ts, D_POOL), lambda b, s: (b, s, 0)),
        out_shape=jax.ShapeDtypeStruct((B, S, D_POOL), BF16),
        scratch_shapes=[pltpu.VMEM((POOL_HALO + ts, D_POOL), F32)],
        compiler_params=_cparams(("arbitrary", "arbitrary")),
        name="pool",
    )(u3, w_bd, scale, g)


def _bias_kernel(rb_ref, o_ref):
    l = pl.program_id(0)
    h = pl.program_id(1)
    n_heads = pl.num_programs(1)
    row = l * n_heads + h
    far = LEFT_CHUNKS * CHUNK - REL_CLIP
    far = (far // LANES) * LANES
    near_w = CA_K - far
    q = lax.broadcasted_iota(jnp.int32, (CA_Q, near_w), 0)
    k = lax.broadcasted_iota(jnp.int32, (CA_Q, near_w), 1) + far
    rel = jnp.clip(q + LEFT_CHUNKS * CHUNK - k, -REL_CLIP, REL_CLIP) + REL_CLIP

    def body(j, acc):
        return jnp.where(rel == j, rb_ref[row, j], acc)

    near = lax.fori_loop(0, 2 * REL_CLIP + 1, body, jnp.zeros((CA_Q, near_w), F32))
    o_ref[0, 0, :, far:] = near
    o_ref[0, 0, :, :far] = jnp.full((CA_Q, far), rb_ref[row, 2 * REL_CLIP], F32)
    qa = lax.broadcasted_iota(jnp.int32, (CA_Q, CA_K), 0) // CHUNK
    ka = lax.broadcasted_iota(jnp.int32, (CA_Q, CA_K), 1) // CHUNK
    in_band = (ka >= qa) & (ka <= qa + LEFT_CHUNKS)
    o_ref[0, 0] = jnp.where(in_band, o_ref[0, 0], NEG_INF)


def _bias_tables(rel_bias):
    L, H, R = rel_bias.shape
    return pl.pallas_call(
        _bias_kernel,
        grid=(L, H),
        in_specs=[pl.BlockSpec(memory_space=pltpu.SMEM)],
        out_specs=pl.BlockSpec((1, 1, CA_Q, CA_K), lambda l, h: (l, h, 0, 0)),
        out_shape=jax.ShapeDtypeStruct((L, H, CA_Q, CA_K), F32),
        compiler_params=_cparams(("parallel", "parallel")),
        name="bias_table",
    )(rel_bias.reshape(L * H, R))


def _chunk_kernel(q_ref, k0_ref, k1_ref, k2_ref, v0_ref, v1_ref, v2_ref, tab_ref, g_ref, o_ref):
    i = pl.program_id(1)
    k_refs = (k0_ref, k1_ref, k2_ref)
    v_refs = (v0_ref, v1_ref, v2_ref)
    lane = lax.broadcasted_iota(jnp.int32, (1, LANES), 1)
    kpos = i * CA_Q - (CA_K - CA_Q) + lax.broadcasted_iota(jnp.int32, (1, CA_K), 1)
    exists = kpos >= 0
    outs = []
    for p in range(D_CHUNK // LANES):
        cols = slice(p * LANES, (p + 1) * LANES)
        qp = q_ref[0, :, cols]
        pair = None
        for hh in range(HEADS_PER_LANE_TILE):
            head = p * HEADS_PER_LANE_TILE + hh
            mine = (lane // HEAD_DIM) == hh
            qm = jnp.where(mine, qp, jnp.zeros_like(qp))
            s = jnp.concatenate([_nt_dot(qm, kr[0, :, cols]) for kr in k_refs], axis=1)
            s = s + tab_ref[head]
            s = jnp.where(exists, s, NEG_INF)
            m = jnp.max(s, axis=-1, keepdims=True)
            e = jnp.exp(s - m)
            den = jnp.sum(e, axis=-1, keepdims=True)
            eb = e.astype(BF16)
            o = jnp.zeros((CA_Q, LANES), F32)
            for j, vr in enumerate(v_refs):
                o = o + jnp.dot(eb[:, j * CA_Q:(j + 1) * CA_Q], vr[0, :, cols], preferred_element_type=F32)
            o = o / den
            pair = o if pair is None else jnp.where(mine, o, pair)
        outs.append(pair)
    y = jnp.concatenate(outs, axis=1)
    yn = y * lax.rsqrt(jnp.mean(y * y, axis=-1, keepdims=True) + RMS_EPS)
    o_ref[0] = (yn * g_ref[...]).astype(o_ref.dtype)


def _chunk_attention(q3, k3, v3, tab, g):
    B, S, _ = q3.shape
    H = tab.shape[0]
    blk = lambda d: pl.BlockSpec((1, CA_Q, D_CHUNK), lambda b, i: (b, jnp.maximum(i - d, 0), 0))
    return pl.pallas_call(
        _chunk_kernel,
        grid=(B, S // CA_Q),
        in_specs=[blk(0), blk(2), blk(1), blk(0), blk(2), blk(1), blk(0),
                  pl.BlockSpec((H, CA_Q, CA_K), lambda b, i: (0, 0, 0)),
                  pl.BlockSpec((1, D_CHUNK), lambda b, i: (0, 0))],
        out_specs=pl.BlockSpec((1, CA_Q, D_CHUNK), lambda b, i: (b, i, 0)),
        out_shape=jax.ShapeDtypeStruct((B, S, D_CHUNK), BF16),
        compiler_params=_cparams(("parallel", "parallel")),
        name="chunk_attn",
    )(q3, k3, k3, k3, v3, v3, v3, tab, g)


def _sb_kernel(q_ref, k_ref, v_ref, tri_ref, g_ref, o_ref):
    i = pl.program_id(1)
    row = lax.broadcasted_iota(jnp.int32, (SB_T, SB_T), 0)
    col = lax.broadcasted_iota(jnp.int32, (SB_T, SB_T), 1)
    before = col < row
    lane = lax.broadcasted_iota(jnp.int32, (1, LANES), 1)
    outs = []
    for p in range(D_SB // LANES):
        cols = slice(p * LANES, (p + 1) * LANES)
        qp = q_ref[0, :, cols]
        pair = None
        for hh in range(HEADS_PER_LANE_TILE):
            mine = (lane // HEAD_DIM) == hh
            qm = jnp.where(mine, qp, jnp.zeros_like(qp))

            def block(j, run, acc, diag):
                start = pl.multiple_of(j * SB_T, SB_T)
                kj = k_ref[0, pl.ds(start, SB_T), cols]
                vj = v_ref[0, pl.ds(start, SB_T), cols]
                z = _nt_dot(qm, kj)
                sp = jnp.maximum(z, 0.0) + jnp.log1p(jnp.exp(-jnp.abs(z)))
                log_keep = -sp
                if diag:
                    log_keep = jnp.where(before, log_keep, 0.0)
                hi = log_keep.astype(BF16)
                lo = (log_keep - hi.astype(F32)).astype(BF16)
                suffix = jnp.dot(jnp.concatenate([hi, lo], axis=1), tri_ref[...], preferred_element_type=F32)
                w = jnp.exp(z - sp + suffix + run)
                if diag:
                    w = jnp.where(before, w, 0.0)
                acc = acc + jnp.dot(w.astype(BF16), vj, preferred_element_type=F32)
                run = run + jnp.sum(log_keep, axis=-1, keepdims=True)
                return run, acc

            run0, acc0 = block(i, jnp.zeros((SB_T, 1), F32), jnp.zeros((SB_T, LANES), F32), True)

            def cond(c):
                return jnp.logical_and(c[0] >= 0, c[3] > SB_DEAD)

            def body(c):
                run, acc = block(c[0], c[1], c[2], False)
                return c[0] - 1, run, acc, jnp.max(run)

            _, _, o, _ = lax.while_loop(cond, body, (i - 1, run0, acc0, jnp.max(run0)))
            pair = o if pair is None else jnp.where(mine, o, pair)
        outs.append(pair)
    y = jnp.concatenate(outs, axis=1)
    yn = y * lax.rsqrt(jnp.mean(y * y, axis=-1, keepdims=True) + RMS_EPS)
    o_ref[0] = (yn * g_ref[...]).astype(o_ref.dtype)


def _sb_attention(q3, k3, v3, g):
    B, S, _ = q3.shape
    r = lax.broadcasted_iota(jnp.int32, (2 * SB_T, SB_T), 0) % SB_T
    c = lax.broadcasted_iota(jnp.int32, (2 * SB_T, SB_T), 1)
    tri = (r > c).astype(BF16)
    full = pl.BlockSpec((1, S, D_SB), lambda b, i: (b, 0, 0))
    return pl.pallas_call(
        _sb_kernel,
        grid=(B, S // SB_T),
        in_specs=[pl.BlockSpec((1, SB_T, D_SB), lambda b, i: (b, i, 0)), full, full,
                  pl.BlockSpec((2 * SB_T, SB_T), lambda b, i: (0, 0)),
                  pl.BlockSpec((1, D_SB), lambda b, i: (0, 0))],
        out_specs=pl.BlockSpec((1, SB_T, D_SB), lambda b, i: (b, i, 0)),
        out_shape=jax.ShapeDtypeStruct((B, S, D_SB), BF16),
        compiler_params=_cparams(("parallel", "parallel")),
        name="sb_attn",
    )(q3, k3, v3, tri, g)


def _outproj_kernel(yp_ref, yc_ref, ys_ref, x_ref, w_ref, g_ref, b_ref, o_ref):
    m = jnp.dot(yp_ref[...], w_ref[0:D_POOL, :], preferred_element_type=F32)
    m = m + jnp.dot(yc_ref[...], w_ref[D_POOL:D_POOL + D_CHUNK, :], preferred_element_type=F32)
    m = m + jnp.dot(ys_ref[...], w_ref[D_POOL + D_CHUNK:, :], preferred_element_type=F32)
    o_ref[...] = _layer_norm(ALPHA * x_ref[...] + m, g_ref[...], b_ref[...])


def _outproj_ln(yp, yc, ys, x2, w_bf, g, b, tm=512):
    T, D = x2.shape
    row = lambda w: pl.BlockSpec((tm, w), lambda i: (i, 0))
    vec = pl.BlockSpec((1, D), lambda i: (0, 0))
    return pl.pallas_call(
        _outproj_kernel,
        grid=(T // tm,),
        in_specs=[row(D_POOL), row(D_CHUNK), row(D_SB), row(D), pl.BlockSpec((D, D), lambda i: (0, 0)), vec, vec],
        out_specs=row(D),
        out_shape=jax.ShapeDtypeStruct((T, D), F32),
        compiler_params=_cparams(("parallel",)),
        name="outproj_ln",
    )(yp, yc, ys, x2, w_bf, g, b)


FF_CHUNK = 256


def _swiglu_rows(xb, wg_ref, wu_ref, wd_ref, lead):
    width = wd_ref.shape[-2]
    y = None
    lo = 0
    while lo < width:
        w = min(FF_CHUNK, width - lo)
        g = jnp.dot(xb, wg_ref[lead + (slice(None), slice(lo, lo + w))], preferred_element_type=F32)
        u = jnp.dot(xb, wu_ref[lead + (slice(None), slice(lo, lo + w))], preferred_element_type=F32)
        h = (g * (1.0 / (1.0 + jnp.exp(-g))) * u).astype(BF16)
        part = jnp.dot(h, wd_ref[lead + (slice(lo, lo + w), slice(None))], preferred_element_type=F32)
        y = part if y is None else y + part
        lo += w
    return y


def _ffn_kernel(x_ref, wg_ref, wu_ref, wd_ref, g_ref, b_ref, o_ref):
    x = x_ref[...]
    f = _swiglu_rows(x.astype(BF16), wg_ref, wu_ref, wd_ref, ())
    o_ref[...] = _layer_norm(ALPHA * x + f, g_ref[...], b_ref[...])


def _ffn_ln(x2, wg, wu, wd, g, b, tm=512):
    T, D = x2.shape
    F = wg.shape[1]
    row = pl.BlockSpec((tm, D), lambda i: (i, 0))
    vec = pl.BlockSpec((1, D), lambda i: (0, 0))
    once = pl.Buffered(1)
    return pl.pallas_call(
        _ffn_kernel,
        grid=(T // tm,),
        in_specs=[row,
                  pl.BlockSpec((D, F), lambda i: (0, 0), pipeline_mode=once),
                  pl.BlockSpec((D, F), lambda i: (0, 0), pipeline_mode=once),
                  pl.BlockSpec((F, D), lambda i: (0, 0), pipeline_mode=once), vec, vec],
        out_specs=row,
        out_shape=jax.ShapeDtypeStruct((T, D), F32),
        compiler_params=_cparams(("parallel",)),
        name="ffn_ln",
    )(x2, wg, wu, wd, g, b)


def _router_kernel(x_ref, r_ref, gate_ref, pos_ref, cnt_ref):
    tm = x_ref.shape[0]
    logits = jnp.dot(x_ref[...], r_ref[...], preferred_element_type=F32, precision=lax.Precision.HIGHEST)
    lane = lax.broadcasted_iota(jnp.int32, (tm, LANES), 1).astype(F32)
    lg = jnp.where(lane < N_EXPERTS, logits, -jnp.inf)
    m1 = jnp.max(lg, axis=-1, keepdims=True)
    i1 = jnp.min(jnp.where(lg == m1, lane, float(LANES)), axis=-1, keepdims=True)
    sel1 = lane == i1
    lg2 = jnp.where(sel1, -jnp.inf, lg)
    m2 = jnp.max(lg2, axis=-1, keepdims=True)
    i2 = jnp.min(jnp.where(lg2 == m2, lane, float(LANES)), axis=-1, keepdims=True)
    sel2 = lane == i2
    e2 = jnp.exp(m2 - m1)
    w1 = 1.0 / (1.0 + e2)
    w2 = e2 / (1.0 + e2)
    gate_ref[...] = jnp.where(sel1, w1, 0.0) + jnp.where(sel2, w2, 0.0)
    chosen = jnp.logical_or(sel1, sel2)
    mask = jnp.where(chosen, 1.0, 0.0)
    r = lax.broadcasted_iota(jnp.int32, (tm, tm), 0)
    c = lax.broadcasted_iota(jnp.int32, (tm, tm), 1)
    earlier = jnp.where(c < r, 1.0, 0.0).astype(BF16)
    rank = jnp.dot(earlier, mask.astype(BF16), preferred_element_type=F32)
    pos_ref[...] = jnp.where(chosen, rank, -1.0)
    cnt = jnp.sum(mask, axis=0, keepdims=True)
    cnt_ref[0] = jnp.broadcast_to(cnt, (8, LANES))


def _router(x2, router_pad, tm):
    T, D = x2.shape
    nb = T // tm
    return pl.pallas_call(
        _router_kernel,
        grid=(nb,),
        in_specs=[pl.BlockSpec((tm, D), lambda i: (i, 0)), pl.BlockSpec((D, LANES), lambda i: (0, 0))],
        out_specs=[pl.BlockSpec((tm, LANES), lambda i: (i, 0)), pl.BlockSpec((tm, LANES), lambda i: (i, 0)),
                   pl.BlockSpec((1, 8, LANES), lambda i: (i, 0, 0))],
        out_shape=[jax.ShapeDtypeStruct((T, LANES), F32), jax.ShapeDtypeStruct((T, LANES), F32),
                   jax.ShapeDtypeStruct((nb, 8, LANES), F32)],
        compiler_params=_cparams(("parallel",)),
        name="router",
    )(x2, router_pad)


def _moe_kernel(cnt_ref, x_ref, post_ref, pos_ref, gate_ref, wg_ref, wu_ref, wd_ref, o_ref, xc_ref, yc_ref):
    b = pl.program_id(0)
    e = pl.program_id(1)
    f = pl.program_id(2)
    nf = pl.num_programs(2)
    tm = x_ref.shape[0]
    n_tiles = (cnt_ref[b * N_EXPERTS + e] + MOE_RT - 1) // MOE_RT

    @pl.when(jnp.logical_and(e == 0, f == 0))
    def _():
        o_ref[...] = jnp.zeros_like(o_ref)

    @pl.when(f == 0)
    def _():
        prow = post_ref[pl.ds(e, 1), :]

        def gather(r, carry):
            base = pl.multiple_of(r * MOE_RT, MOE_RT)
            slot = (base + lax.broadcasted_iota(jnp.int32, (MOE_RT, tm), 0)).astype(F32)
            onehot = jnp.where(prow == slot, 1.0, 0.0).astype(BF16)
            xc_ref[pl.ds(base, MOE_RT), :] = jnp.dot(onehot, x_ref[...], preferred_element_type=F32).astype(BF16)
            return carry

        lax.fori_loop(0, n_tiles, gather, 0)

    def expert(r, carry):
        base = pl.multiple_of(r * MOE_RT, MOE_RT)
        y = _swiglu_rows(xc_ref[pl.ds(base, MOE_RT), :], wg_ref, wu_ref, wd_ref, (0,))

        @pl.when(f == 0)
        def _():
            yc_ref[pl.ds(base, MOE_RT), :] = y

        @pl.when(f != 0)
        def _():
            yc_ref[pl.ds(base, MOE_RT), :] += y

        return carry

    lax.fori_loop(0, n_tiles, expert, 0)

    @pl.when(f == nf - 1)
    def _():
        lane = lax.broadcasted_iota(jnp.int32, (tm, LANES), 1)
        pick = lane == e
        pcol = jnp.sum(jnp.where(pick, pos_ref[...], 0.0), axis=-1, keepdims=True)
        gcol = jnp.sum(jnp.where(pick, gate_ref[...], 0.0), axis=-1, keepdims=True)

        def scatter(r, carry):
            base = pl.multiple_of(r * MOE_RT, MOE_RT)
            yb = yc_ref[pl.ds(base, MOE_RT), :].astype(BF16)
            slot = (base + lax.broadcasted_iota(jnp.int32, (MOE_RT, MOE_RT), 1)).astype(F32)
            for t0 in range(0, tm, MOE_RT):
                onehot = jnp.where(pcol[t0:t0 + MOE_RT] == slot, 1.0, 0.0).astype(BF16)
                part = jnp.dot(onehot, yb, preferred_element_type=F32)
                o_ref[t0:t0 + MOE_RT, :] += gcol[t0:t0 + MOE_RT] * part
            return carry

        lax.fori_loop(0, n_tiles, scatter, 0)


def _moe(xb, post, pos, gate, cnt, wg, wu, wd, tm):
    T, D = xb.shape
    E, _, F = wg.shape
    nf = 2
    tf = F // nf
    nb = T // tm
    grid_spec = pltpu.PrefetchScalarGridSpec(
        num_scalar_prefetch=1,
        grid=(nb, E, nf),
        in_specs=[pl.BlockSpec((tm, D), lambda b, e, f, c: (b, 0)),
                  pl.BlockSpec((8, tm), lambda b, e, f, c: (0, b)),
                  pl.BlockSpec((tm, LANES), lambda b, e, f, c: (b, 0)),
                  pl.BlockSpec((tm, LANES), lambda b, e, f, c: (b, 0)),
                  pl.BlockSpec((1, D, tf), lambda b, e, f, c: (e, 0, f)),
                  pl.BlockSpec((1, D, tf), lambda b, e, f, c: (e, 0, f)),
                  pl.BlockSpec((1, tf, D), lambda b, e, f, c: (e, f, 0))],
        out_specs=pl.BlockSpec((tm, D), lambda b, e, f, c: (b, 0)),
        scratch_shapes=[pltpu.VMEM((tm, D), BF16), pltpu.VMEM((tm, D), F32)],
    )
    return pl.pallas_call(
        _moe_kernel,
        grid_spec=grid_spec,
        out_shape=jax.ShapeDtypeStruct((T, D), F32),
        compiler_params=_cparams(("parallel", "arbitrary", "arbitrary")),
        name="moe",
    )(cnt, xb, post, pos, gate, wg, wu, wd)


def _resln_kernel(x_ref, f_ref, g_ref, b_ref, o_ref):
    o_ref[...] = _layer_norm(ALPHA * x_ref[...] + f_ref[...], g_ref[...], b_ref[...])


def _res_ln(x2, f2, g, b, tm=1024):
    T, D = x2.shape
    row = pl.BlockSpec((tm, D), lambda i: (i, 0))
    vec = pl.BlockSpec((1, D), lambda i: (0, 0))
    return pl.pallas_call(
        _resln_kernel,
        grid=(T // tm,),
        in_specs=[row, row, vec, vec],
        out_specs=row,
        out_shape=jax.ShapeDtypeStruct((T, D), F32),
        compiler_params=_cparams(("parallel",)),
        name="res_ln",
    )(x2, f2, g, b)


def _moe_ln(x2, router, wg, wu, wd, g, b):
    T, D = x2.shape
    tm = min(MOE_TM, T)
    router_pad = jnp.pad(router, ((0, 0), (0, LANES - N_EXPERTS)))
    gate, pos, cnt = _router(x2, router_pad, tm)
    post = pos[:, :8].T
    counts = cnt[:, 0, :N_EXPERTS].astype(jnp.int32).reshape(-1)
    f = _moe(x2.astype(BF16), post, pos, gate, counts, wg, wu, wd, tm)
    return _res_ln(x2, f, g, b, tm)


def _block_diag(pool_w):
    G, C, _ = pool_w.shape
    out = jnp.zeros((G * C, G * C), pool_w.dtype)
    for gi in range(G):
        out = out.at[gi * C:(gi + 1) * C, gi * C:(gi + 1) * C].set(pool_w[gi])
    return out


def kernel(x, w_in, pool_w, pool_scale, rel_bias, g_mix, w_out, ln1_g, ln1_b, ln2_g, ln2_b,
           ffn_wg, ffn_wu, ffn_wd, moe_router, moe_wg, moe_wu, moe_wd):
    B, S, D = x.shape
    T = B * S
    depth = w_in.shape[0]
    x2 = x.reshape(T, D)
    tabs = _bias_tables(rel_bias)
    vec = lambda a: a.reshape(1, -1)
    for i in range(depth):
        u, qc, kc, vc, qs, ks, vs = _inproj(x2, w_in[i].astype(BF16))
        g_pool = vec(g_mix[i, :D_POOL])
        g_chunk = vec(g_mix[i, D_POOL:D_POOL + D_CHUNK])
        g_sb = vec(g_mix[i, D_POOL + D_CHUNK:])
        r3 = lambda a: a.reshape(B, S, a.shape[-1])
        yp = _pool(r3(u), _block_diag(pool_w[i]).astype(BF16), vec(pool_scale[i]), g_pool)
        yc = _chunk_attention(r3(qc), r3(kc), r3(vc), tabs[i], g_chunk)
        ys = _sb_attention(r3(qs), r3(ks), r3(vs), g_sb)
        x2 = _outproj_ln(yp.reshape(T, -1), yc.reshape(T, -1), ys.reshape(T, -1), x2,
                         w_out[i].astype(BF16), vec(ln1_g[i]), vec(ln1_b[i]))
        j = i // 2
        if i % 2 == 0:
            x2 = _ffn_ln(x2, ffn_wg[j].astype(BF16), ffn_wu[j].astype(BF16), ffn_wd[j].astype(BF16),
                         vec(ln2_g[i]), vec(ln2_b[i]))
        else:
            x2 = _moe_ln(x2, moe_router[j], moe_wg[j].astype(BF16), moe_wu[j].astype(BF16),
                         moe_wd[j].astype(BF16), vec(ln2_g[i]), vec(ln2_b[i]))
    return x2.reshape(B, S, D)
```
